```python
import functools
import jax, jax.numpy as jnp
from jax import lax
import numpy as np

D_MODEL = 1024
BATCH = 4
SEQ = 8192
DEPTH = 1
DEC_BATCH = 16
DEC_SEQ = 16
PAST_LEN = 2048

CHUNK = 64
D_MIX = D_MODEL
D_ATT = D_MIX // 2
HEAD_DIM = 64
N_HEADS = D_ATT // HEAD_DIM
D_CONV = D_MIX - D_ATT
CONV_WIDTH = 31
CONV_STATE = CONV_WIDTH - 1
D_FF = 2816
Q_BLOCK = 128
N_MOD = 9
EPS = 1e-6
D_IN = 3 * D_ATT + N_HEADS + 2 * D_CONV

kernel_name = "fox_conformer_hybrid_stream_step"


def _rms(x, g):
    xf = x.astype(jnp.float32)
    y = xf * lax.rsqrt(jnp.mean(xf * xf, axis=-1, keepdims=True) + EPS)
    return (y * g.astype(jnp.float32)).astype(x.dtype)


def _layernorm(x, g, b):
    xf = x.astype(jnp.float32)
    mu = jnp.mean(xf, axis=-1, keepdims=True)
    var = jnp.mean(jnp.square(xf - mu), axis=-1, keepdims=True)
    return ((xf - mu) * lax.rsqrt(var + EPS) * g.astype(jnp.float32) + b.astype(jnp.float32)).astype(x.dtype)


def _modulate(x, g, shift, scale):
    return _rms(x, g) * (1 + scale[:, None, :]) + shift[:, None, :]


def _adaln(c, w_ada, b_ada):
    return jnp.split(jax.nn.silu(c) @ w_ada + b_ada, N_MOD, axis=-1)


def _swiglu(h, w_up, w_down):
    a, b = jnp.split(h @ w_up, 2, axis=-1)
    return (jax.nn.silu(a) * b) @ w_down


def _project_in(h, w_in, b_f, g_q, g_k):
    B, T = h.shape[:2]
    z = h @ w_in
    q, k, v, fl, u = jnp.split(z, [D_ATT, 2 * D_ATT, 3 * D_ATT, 3 * D_ATT + N_HEADS], axis=-1)
    q = _rms(q.reshape(B, T, N_HEADS, HEAD_DIM), g_q)
    k = _rms(k.reshape(B, T, N_HEADS, HEAD_DIM), g_k)
    v = v.reshape(B, T, N_HEADS, HEAD_DIM)
    logf = jax.nn.log_sigmoid((fl + b_f).astype(jnp.float32))
    a, gt = jnp.split(u, 2, axis=-1)
    u = a * jax.nn.sigmoid(gt)
    return q, k, v, logf, u


def _fox_attend(q, F_q, k, v, F_k, q_pos, k_pos):
    s = jnp.einsum('bqhd,bkhd->bhqk', q, k, preferred_element_type=jnp.float32) * (HEAD_DIM ** -0.5)
    s = s + (jnp.swapaxes(F_q, 1, 2)[..., :, None] - jnp.swapaxes(F_k, 1, 2)[..., None, :])
    s = jnp.where(k_pos[None, :] <= q_pos[:, None], s, -jnp.inf)
    p = jax.nn.softmax(s, axis=-1)
    return jnp.einsum('bhqk,bkhd->bqhd', p.astype(v.dtype), v)


def _fox_prompt(q, k, v, logf):
    B, T = q.shape[:2]
    F = jnp.cumsum(logf, axis=1)
    nb = T // Q_BLOCK
    qb = q.reshape(B, nb, Q_BLOCK, N_HEADS, HEAD_DIM).swapaxes(0, 1)
    Fb = F.reshape(B, nb, Q_BLOCK, N_HEADS).swapaxes(0, 1)
    k_pos = jnp.arange(T)

    def block(args):
        qi, Fi, i = args
        q_pos = i * Q_BLOCK + jnp.arange(Q_BLOCK)
        return _fox_attend(qi, Fi, k, v, F, q_pos, k_pos)

    o = lax.map(block, (qb, Fb, jnp.arange(nb)))
    return o.swapaxes(0, 1).reshape(B, T, N_HEADS, HEAD_DIM)


def _fox_sample(q, k, v, logf, cache_k, cache_v, cache_logf):
    L, T = cache_k.shape[1], q.shape[1]
    kk = jnp.concatenate([cache_k.astype(k.dtype), k], axis=1)
    vv = jnp.concatenate([cache_v.astype(v.dtype), v], axis=1)
    F = jnp.cumsum(jnp.concatenate([cache_logf.astype(jnp.float32), logf], axis=1), axis=1)
    q_pos = L + jnp.arange(T)
    k_pos = jnp.arange(L + T)
    return _fox_attend(q, F[:, L:], kk, vv, F, q_pos, k_pos)


def _dwconv(u_pad, conv_w, conv_b):
    y = lax.conv_general_dilated(u_pad, conv_w[:, None, :].astype(u_pad.dtype), window_strides=(1,),
                                 padding='VALID', dimension_numbers=('NWC', 'WIO', 'NWC'),
                                 feature_group_count=D_CONV)
    return y + conv_b


def _mix_out(o, u_pad, conv_w, conv_b, conv_ln_g, conv_ln_b, w_out):
    B, T = o.shape[:2]
    y = jax.nn.silu(_layernorm(_dwconv(u_pad, conv_w, conv_b), conv_ln_g, conv_ln_b))
    return jnp.concatenate([o.reshape(B, T, D_ATT), y], axis=-1) @ w_out


def _mixer_prompt(h, w_in, b_f, g_q, g_k, conv_w, conv_b, conv_ln_g, conv_ln_b, w_out):
    q, k, v, logf, u = _project_in(h, w_in, b_f, g_q, g_k)
    o = _fox_prompt(q, k, v, logf)
    u_pad = jnp.concatenate([jnp.zeros((u.shape[0], CONV_STATE, D_CONV), u.dtype), u], axis=1)
    out = _mix_out(o, u_pad, conv_w, conv_b, conv_ln_g, conv_ln_b, w_out)
    return out, (k, v, logf, u_pad[:, -CONV_STATE:])


def _mixer_sample(h, cache_k, cache_v, cache_logf, state_conv, w_in, b_f, g_q, g_k,
                  conv_w, conv_b, conv_ln_g, conv_ln_b, w_out):
    q, k, v, logf, u = _project_in(h, w_in, b_f, g_q, g_k)
    o = _fox_sample(q, k, v, logf, cache_k, cache_v, cache_logf)
    u_pad = jnp.concatenate([state_conv.astype(u.dtype), u], axis=1)
    out = _mix_out(o, u_pad, conv_w, conv_b, conv_ln_g, conv_ln_b, w_out)
    return out, (k, v, logf, u_pad[:, -CONV_STATE:])


def _layer(x, c, mixer_fn, w_ada, b_ada, g_ffn1, w_up1, w_down1, g_mix, g_ffn2, w_up2, w_down2, g_final):
    sh1, sc1, gt1, sh2, sc2, gt2, sh3, sc3, gt3 = _adaln(c, w_ada, b_ada)
    x = x + 0.5 * gt1[:, None, :] * _swiglu(_modulate(x, g_ffn1, sh1, sc1), w_up1, w_down1)
    m, states = mixer_fn(_modulate(x, g_mix, sh2, sc2))
    x = x + gt2[:, None, :] * m
    x = x + 0.5 * gt3[:, None, :] * _swiglu(_modulate(x, g_ffn2, sh3, sc3), w_up2, w_down2)
    return _rms(x, g_final), states


def setup_inputs(seed: int = 0) -> dict:
    key = jax.random.key(seed)
    ks = jax.random.split(key, 32)
    f32 = jnp.float32
    nrm = lambda k, shape, s: jax.random.normal(k, shape, f32) * s
    gain = lambda k, shape: 1.0 + 0.02 * jax.random.normal(k, shape, f32)
    return {
        "x_prompt": nrm(ks[0], (BATCH, SEQ, D_MODEL), 1.0),
        "x_sample": nrm(ks[1], (DEC_BATCH, DEC_SEQ, D_MODEL), 1.0),
        "c_prompt": nrm(ks[2], (BATCH, D_MODEL), 1.0),
        "c_sample": nrm(ks[3], (DEC_BATCH, D_MODEL), 1.0),
        "cache_k": nrm(ks[4], (DEPTH, DEC_BATCH, PAST_LEN, N_HEADS, HEAD_DIM), 1.0),
        "cache_v": nrm(ks[5], (DEPTH, DEC_BATCH, PAST_LEN, N_HEADS, HEAD_DIM), 1.0),
        "cache_logf": jax.nn.log_sigmoid(4.0 + jax.random.normal(ks[6], (DEPTH, DEC_BATCH, PAST_LEN, N_HEADS), f32)),
        "state_conv": nrm(ks[7], (DEPTH, DEC_BATCH, CONV_STATE, D_CONV), 0.5),
        "w_ada": nrm(ks[8], (DEPTH, D_MODEL, N_MOD * D_MODEL), 0.5 * D_MODEL ** -0.5),
        "b_ada": nrm(ks[9], (DEPTH, N_MOD * D_MODEL), 0.02),
        "g_ffn1": gain(ks[10], (DEPTH, D_MODEL)),
        "w_up1": nrm(ks[11], (DEPTH, D_MODEL, 2 * D_FF), D_MODEL ** -0.5),
        "w_down1": nrm(ks[12], (DEPTH, D_FF, D_MODEL), D_FF ** -0.5),
        "g_mix": gain(ks[13], (DEPTH, D_MODEL)),
        "w_in": nrm(ks[14], (DEPTH, D_MODEL, D_IN), D_MODEL ** -0.5),
        "b_f": 2.0 + 4.0 * jax.random.uniform(ks[15], (DEPTH, N_HEADS), f32),
        "g_q": gain(ks[16], (DEPTH, HEAD_DIM)),
        "g_k": gain(ks[17], (DEPTH, HEAD_DIM)),
        "conv_w": nrm(ks[18], (DEPTH, CONV_WIDTH, D_CONV), CONV_WIDTH ** -0.5),
        "conv_b": nrm(ks[19], (DEPTH, D_CONV), 0.02),
        "conv_ln_g": gain(ks[20], (DEPTH, D_CONV)),
        "conv_ln_b": nrm(ks[21], (DEPTH, D_CONV), 0.02),
        "w_out": nrm(ks[22], (DEPTH, D_MIX, D_MODEL), D_MIX ** -0.5),
        "g_ffn2": gain(ks[23], (DEPTH, D_MODEL)),
        "w_up2": nrm(ks[24], (DEPTH, D_MODEL, 2 * D_FF), D_MODEL ** -0.5),
        "w_down2": nrm(ks[25], (DEPTH, D_FF, D_MODEL), D_FF ** -0.5),
        "g_final": gain(ks[26], (DEPTH, D_MODEL)),
    }


def reference(x_prompt, x_sample, c_prompt, c_sample, cache_k, cache_v, cache_logf, state_conv,
              w_ada, b_ada, g_ffn1, w_up1, w_down1, g_mix, w_in, b_f, g_q, g_k,
              conv_w, conv_b, conv_ln_g, conv_ln_b, w_out, g_ffn2, w_up2, w_down2, g_final):
    xp, xs = x_prompt, x_sample
    kp, vp, fp, cp, ksm, vsm, fsm, csm = [], [], [], [], [], [], [], []
    for l in range(DEPTH):
        mix_w = (w_in[l], b_f[l], g_q[l], g_k[l], conv_w[l], conv_b[l], conv_ln_g[l], conv_ln_b[l], w_out[l])
        layer_w = (w_ada[l], b_ada[l], g_ffn1[l], w_up1[l], w_down1[l], g_mix[l],
                   g_ffn2[l], w_up2[l], w_down2[l], g_final[l])
        prompt_mixer = functools.partial(_mixer_prompt, w_in=mix_w[0], b_f=mix_w[1], g_q=mix_w[2], g_k=mix_w[3],
                                         conv_w=mix_w[4], conv_b=mix_w[5], conv_ln_g=mix_w[6],
                                         conv_ln_b=mix_w[7], w_out=mix_w[8])
        sample_mixer = functools.partial(_mixer_sample, cache_k=cache_k[l], cache_v=cache_v[l],
                                         cache_logf=cache_logf[l], state_conv=state_conv[l],
                                         w_in=mix_w[0], b_f=mix_w[1], g_q=mix_w[2], g_k=mix_w[3],
                                         conv_w=mix_w[4], conv_b=mix_w[5], conv_ln_g=mix_w[6],
                                         conv_ln_b=mix_w[7], w_out=mix_w[8])
        xp, (k1, v1, f1, c1) = _layer(xp, c_prompt, prompt_mixer, *layer_w)
        xs, (k2, v2, f2, c2) = _layer(xs, c_sample, sample_mixer, *layer_w)
        kp.append(k1); vp.append(v1); fp.append(f1); cp.append(c1)
        ksm.append(k2); vsm.append(v2); fsm.append(f2); csm.append(c2)
    k_prompt, v_prompt = jnp.stack(kp, 0), jnp.stack(vp, 0)
    logf_prompt, conv_prompt = jnp.stack(fp, 0), jnp.stack(cp, 0)
    k_sample, v_sample = jnp.stack(ksm, 0), jnp.stack(vsm, 0)
    logf_sample, conv_sample = jnp.stack(fsm, 0), jnp.stack(csm, 0)
    return (xp, xs, k_prompt, v_prompt, logf_prompt, conv_prompt, k_sample, v_sample, logf_sample, conv_sample)
```

```python
import functools

import jax
import jax.numpy as jnp
import numpy as np
from jax import lax
from jax.experimental import pallas as pl
from jax.experimental.pallas import tpu as pltpu

F32 = jnp.float32
BF16 = jnp.bfloat16

D_MODEL = 1024
D_ATT = 512
HEAD_DIM = 64
N_HEADS = 8
N_PAIRS = N_HEADS // 2
D_CONV = 512
CONV_WIDTH = 31
CONV_STATE = CONV_WIDTH - 1
D_FF = 2816
N_MOD = 9
EPS = 1e-6
LOG2E = 1.4426950408889634
NEG_BIG = -1e30

LANES = 128
SUBLANES = 8
HALO = 32
FF_CHUNK = 256
CUM_BLK = 128
F_PARTS = 3
ONE_LANE = F_PARTS * N_HEADS
VMEM_LIMIT = 56 * 1024 * 1024


def _cparams(n_axes):
    return pltpu.CompilerParams(dimension_semantics=("arbitrary",) * n_axes,
                                vmem_limit_bytes=VMEM_LIMIT)


def _const_spec(shape):
    nd = len(shape)
    return pl.BlockSpec(shape, lambda *_: (0,) * nd, pipeline_mode=pl.Buffered(1))


def _dot(a, b):
    return jnp.dot(a, b, preferred_element_type=F32)


def _dot_nt(a, b):
    return lax.dot_general(a, b, (((1,), (1,)), ((), ())), preferred_element_type=F32)


def _rms(x, g):
    return x * lax.rsqrt(jnp.mean(x * x, axis=-1, keepdims=True) + EPS) * g


def _split3(x):
    p1 = x.astype(BF16)
    r1 = x - p1.astype(F32)
    p2 = r1.astype(BF16)
    p3 = (r1 - p2.astype(F32)).astype(BF16)
    return p1, p2, p3


def _pack_forget(f3):
    p1, p2, p3 = _split3(f3)
    lane = lax.broadcasted_iota(jnp.int32, f3.shape, 1)
    one = jnp.where(lane == ONE_LANE, 1.0, 0.0).astype(BF16)
    return jnp.where(lane < N_HEADS, p1,
                     jnp.where(lane < 2 * N_HEADS, p2,
                               jnp.where(lane < 3 * N_HEADS, p3, one)))


def _cumsum_rows(lf, ltri, carry):
    outs = []
    for sb in range(lf.shape[0] // CUM_BLK):
        p1, p2, p3 = _split3(lf[sb * CUM_BLK:(sb + 1) * CUM_BLK])
        c = _dot(ltri, p1) + _dot(ltri, p2) + _dot(ltri, p3)
        if carry is not None:
            c = c + carry
            carry = c[CUM_BLK - 1:CUM_BLK, :]
        outs.append(c)
    return jnp.concatenate(outs, axis=0), carry


def _ada_kernel(c_ref, w_ref, b_ref, o_ref):
    c = c_ref[...]
    s = (c * jax.nn.sigmoid(c)).astype(BF16)
    o_ref[...] = _dot(s, w_ref[...].astype(BF16)) + b_ref[...]


def _adaln(c_all, w_ada, b_ada):
    n = c_all.shape[0]
    tn = D_MODEL
    return pl.pallas_call(
        _ada_kernel,
        grid=(N_MOD * D_MODEL // tn,),
        in_specs=[pl.BlockSpec((n, D_MODEL), lambda j: (0, 0)),
                  pl.BlockSpec((D_MODEL, tn), lambda j: (0, j)),
                  pl.BlockSpec((1, tn), lambda j: (0, j))],
        out_specs=pl.BlockSpec((n, tn), lambda j: (0, j)),
        out_shape=jax.ShapeDtypeStruct((n, N_MOD * D_MODEL), F32),
        compiler_params=_cparams(1),
        name="adaln",
    )(c_all, w_ada, b_ada.reshape(1, -1))


def _mod_spec(arr, tm):
    if arr.shape[1] == 1:
        return pl.BlockSpec((None, 1, D_MODEL), lambda b, t: (b, 0, 0))
    return pl.BlockSpec((None, tm, D_MODEL), lambda b, t: (b, t, 0))


def _ffn_kernel(x_ref, g_ref, sh_ref, sc_ref, gt_ref, wup_ref, wdn_ref, gfin_ref, o_ref, gs_ref, *, final):
    x = x_ref[...]
    h = (_rms(x, g_ref[...]) * (1.0 + sc_ref[...]) + sh_ref[...]).astype(BF16)
    for c in range(D_FF // FF_CHUNK):
        lo = c * FF_CHUNK
        a = _dot(h, wup_ref[:, lo:lo + FF_CHUNK])
        b = _dot(h, wup_ref[:, D_FF + lo:D_FF + lo + FF_CHUNK])
        gs_ref[:, lo:lo + FF_CHUNK] = (a * jax.nn.sigmoid(a) * b).astype(BF16)
    y = _dot(gs_ref[...], wdn_ref[...])
    out = x + 0.5 * gt_ref[...] * y
    if final:
        out = _rms(out, gfin_ref[...])
    o_ref[...] = out


def _ffn(x, g, sh, sc, gt, w_up, w_dn, g_final, *, tm, final):
    nb, t, _ = x.shape
    xspec = pl.BlockSpec((None, tm, D_MODEL), lambda b, i: (b, i, 0))
    return pl.pallas_call(
        functools.partial(_ffn_kernel, final=final),
        grid=(nb, t // tm),
        in_specs=[xspec, _const_spec((1, D_MODEL)), _mod_spec(sh, tm), _mod_spec(sc, tm), _mod_spec(gt, tm),
                  _const_spec(w_up.shape), _const_spec(w_dn.shape), _const_spec((1, D_MODEL))],
        out_specs=xspec,
        out_shape=jax.ShapeDtypeStruct(x.shape, F32),
        scratch_shapes=[pltpu.VMEM((tm, D_FF), BF16)],
        compiler_params=_cparams(2),
        name="ffn_final" if final else "ffn",
    )(x, g, sh, sc, gt, w_up, w_dn, g_final)


def _mixin_kernel(x_ref, g_ref, sh_ref, sc_ref, wqkv_ref, wf_ref, wu_ref, bf_ref, gq_ref, gk_ref,
                  mblk_ref, ltri_ref, sk_ref,
                  qs_ref, k_ref, v_ref, logf_ref, u_ref, *rest, prompt):
    if prompt:
        kx_ref, vx_ref, p_ref, carry_ref = rest
    else:
        (floc_ref,) = rest
    x = x_ref[...]
    h = (_rms(x, g_ref[...]) * (1.0 + sc_ref[...]) + sh_ref[...]).astype(BF16)

    qkv = _dot(h, wqkv_ref[...])
    q = qkv[:, 0:D_ATT]
    k = qkv[:, D_ATT:2 * D_ATT]
    v = qkv[:, 2 * D_ATT:3 * D_ATT]
    mblk = mblk_ref[...]
    qn = q * lax.rsqrt(_dot((q * q).astype(BF16), mblk) + EPS) * gq_ref[...]
    kn = k * lax.rsqrt(_dot((k * k).astype(BF16), mblk) + EPS) * gk_ref[...]
    qs_ref[...] = (qn * (LOG2E * HEAD_DIM ** -0.5)).astype(BF16)
    k_ref[...] = kn
    v_ref[...] = v

    au = _dot(h, wu_ref[...])
    u_ref[...] = au[:, 0:D_CONV] * jax.nn.sigmoid(au[:, D_CONV:2 * D_CONV])

    lf = jax.nn.log_sigmoid(_dot(h, wf_ref[...]) + bf_ref[...])
    logf_ref[...] = lf[:, 0:N_HEADS]

    if prompt:
        @pl.when(pl.program_id(1) == 0)
        def _():
            carry_ref[...] = jnp.zeros_like(carry_ref)
        fcum, carry = _cumsum_rows(lf, ltri_ref[...], carry_ref[0:1, :])
        carry_ref[0:1, :] = carry
        p = _pack_forget(fcum * LOG2E)
        p_ref[...] = p
        e = _dot(p, sk_ref[...]).astype(BF16)
        lane = lax.broadcasted_iota(jnp.int32, (x.shape[0], LANES), 1)
        first = lane < HEAD_DIM
        for g in range(N_PAIRS):
            sl = slice(g * LANES, (g + 1) * LANES)
            kx_ref[g, :, 0:LANES] = kn[:, sl].astype(BF16)
            kx_ref[g, :, LANES:2 * LANES] = e[:, sl]
            vp = v[:, sl]
            vx_ref[g, :, 0:LANES] = jnp.where(first, vp, 1.0).astype(BF16)
            vx_ref[g, :, LANES:2 * LANES] = jnp.where(first, 1.0, vp).astype(BF16)
    else:
        fcum, _ = _cumsum_rows(lf, ltri_ref[...], None)
        floc_ref[...] = fcum


def _mixin(x, g, sh, sc, w_qkv, w_f, w_u, b_f, gq, gk, mblk, ltri, sk, *, tm, prompt):
    nb, t, _ = x.shape
    row = lambda w: pl.BlockSpec((None, tm, w), lambda b, i: (b, i, 0))
    out_shape = [jax.ShapeDtypeStruct((nb, t, D_ATT), BF16),
                 jax.ShapeDtypeStruct((nb, t, D_ATT), F32),
                 jax.ShapeDtypeStruct((nb, t, D_ATT), F32),
                 jax.ShapeDtypeStruct((nb, t, N_HEADS), F32),
                 jax.ShapeDtypeStruct((nb, t, D_CONV), F32)]
    out_specs = [row(D_ATT), row(D_ATT), row(D_ATT), row(N_HEADS), row(D_CONV)]
    scratch = []
    if prompt:
        pair = pl.BlockSpec((None, N_PAIRS, tm, 2 * LANES), lambda b, i: (b, 0, i, 0))
        out_shape += [jax.ShapeDtypeStruct((nb, N_PAIRS, t, 2 * LANES), BF16),
                      jax.ShapeDtypeStruct((nb, N_PAIRS, t, 2 * LANES), BF16),
                      jax.ShapeDtypeStruct((nb, t, LANES), BF16)]
        out_specs += [pair, pair, row(LANES)]
        scratch = [pltpu.VMEM((SUBLANES, LANES), F32)]
    else:
        out_shape += [jax.ShapeDtypeStruct((nb, t, LANES), F32)]
        out_specs += [row(LANES)]
    return pl.pallas_call(
        functools.partial(_mixin_kernel, prompt=prompt),
        grid=(nb, t // tm),
        in_specs=[row(D_MODEL), _const_spec((1, D_MODEL)), _mod_spec(sh, tm), _mod_spec(sc, tm),
                  _const_spec(w_qkv.shape), _const_spec(w_f.shape), _const_spec(w_u.shape),
                  _const_spec(b_f.shape), _const_spec(gq.shape), _const_spec(gk.shape),
                  _const_spec(mblk.shape), _const_spec(ltri.shape), _const_spec(sk.shape)],
        out_specs=out_specs,
        out_shape=out_shape,
        scratch_shapes=scratch,
        compiler_params=_cparams(2),
        name="mixin_prompt" if prompt else "mixin_sample",
    )(x, g, sh, sc, w_qkv, w_f, w_u, b_f, gq, gk, mblk, ltri, sk)


def _attn_kernel(q_ref, p_ref, sq_ref, kx_ref, vx_ref, o_ref, m_ref, acc_ref, *, tq):
    i = pl.program_id(2)
    lane = lax.broadcasted_iota(jnp.int32, (tq, LANES), 1)
    first = lane < HEAD_DIM
    qp = q_ref[...]
    p = p_ref[...]
    zero = jnp.zeros_like(qp)
    qx = [jnp.concatenate([jnp.where(first, qp, zero), _dot(p, sq_ref[0]).astype(BF16)], axis=1),
          jnp.concatenate([jnp.where(first, zero, qp), _dot(p, sq_ref[1]).astype(BF16)], axis=1)]
    m_ref[...] = jnp.full_like(m_ref, NEG_BIG)
    acc_ref[...] = jnp.zeros_like(acc_ref)

    def step(j, masked):
        start = pl.multiple_of(j * tq, tq)
        kx = kx_ref[pl.ds(start, tq), :]
        vx = vx_ref[pl.ds(start, tq), :]
        for hd in range(2):
            s = _dot_nt(qx[hd], kx)
            if masked:
                r = lax.broadcasted_iota(jnp.int32, s.shape, 0)
                c = lax.broadcasted_iota(jnp.int32, s.shape, 1)
                s = jnp.where(c <= r, s, NEG_BIG)
            m_old = m_ref[hd]
            m_new = jnp.maximum(m_old, jnp.max(s, axis=-1, keepdims=True))
            pr = jnp.exp2(s - m_new).astype(BF16)
            acc_ref[hd] = acc_ref[hd] * jnp.exp2(m_old - m_new) + _dot(pr, vx[:, hd * LANES:(hd + 1) * LANES])
            m_ref[hd] = m_new

    def body(j, carry):
        step(j, False)
        return carry

    lax.fori_loop(0, i, body, 0)
    step(i, True)

    acc_a = acc_ref[0]
    acc_b = acc_ref[1]
    num = jnp.where(first, acc_a, acc_b)
    den = pltpu.roll(jnp.where(first, acc_b, acc_a), HEAD_DIM, 1)
    o_ref[...] = (num / den).astype(o_ref.dtype)


def _attention(qs, p, sq, kx, vx, *, tq):
    nb, t, _ = qs.shape
    blk = lambda b, g, i: (b, i, g)
    seq = pl.BlockSpec((None, None, t, 2 * LANES), lambda b, g, i: (b, g, 0, 0))
    return pl.pallas_call(
        functools.partial(_attn_kernel, tq=tq),
        grid=(nb, N_PAIRS, t // tq),
        in_specs=[pl.BlockSpec((None, tq, LANES), blk),
                  pl.BlockSpec((None, tq, LANES), lambda b, g, i: (b, i, 0)),
                  pl.BlockSpec((2, LANES, LANES), lambda b, g, i: (g, 0, 0)),
                  seq, seq],
        out_specs=pl.BlockSpec((None, tq, LANES), blk),
        out_shape=jax.ShapeDtypeStruct((nb, t, D_ATT), BF16),
        scratch_shapes=[pltpu.VMEM((2, tq, 1), F32), pltpu.VMEM((2, tq, LANES), F32)],
        compiler_params=_cparams(3),
        name="fox_attention",
    )(qs, p, sq, kx, vx)


def _cumsum_kernel(x_ref, ltri_ref, o_ref):
    out, _ = _cumsum_rows(x_ref[...], ltri_ref[...], jnp.zeros((1, LANES), F32))
    o_ref[...] = out


def _cumsum(x, ltri):
    return pl.pallas_call(
        _cumsum_kernel,
        out_shape=jax.ShapeDtypeStruct(x.shape, F32),
        compiler_params=pltpu.CompilerParams(vmem_limit_bytes=VMEM_LIMIT),
        name="cache_cumsum",
    )(x, ltri)


def _sattn_kernel(q_ref, kn_ref, vn_ref, fl_ref, ck_ref, cv_ref, fc_ref, sks_ref, sqs_ref, o_ref, *, tn):
    fc = fc_ref[...]
    fn = fl_ref[...] + fc[fc.shape[0] - 1:, :]
    pc = _pack_forget(fc * LOG2E)
    pn = _pack_forget(fn * LOG2E)
    sks = sks_ref[...]
    kc = jnp.concatenate([ck_ref[...].astype(BF16), _dot(pc, sks).astype(BF16)], axis=1)
    kn = jnp.concatenate([kn_ref[...].astype(BF16), _dot(pn, sks).astype(BF16)], axis=1)
    q = q_ref[...]
    head = lax.broadcasted_iota(jnp.int32, q.shape, 1) // HEAD_DIM
    zero = jnp.zeros_like(q)
    qm = jnp.concatenate(
        [jnp.concatenate([jnp.where(head == h, q, zero), _dot(pn, sqs_ref[h]).astype(BF16)], axis=1)
         for h in range(N_HEADS)], axis=0)
    sc = _dot_nt(qm, kc)
    sn = _dot_nt(qm, kn)
    r = lax.broadcasted_iota(jnp.int32, sn.shape, 0) % tn
    c = lax.broadcasted_iota(jnp.int32, sn.shape, 1)
    sn = jnp.where(c <= r, sn, NEG_BIG)
    m = jnp.maximum(jnp.max(sc, axis=-1, keepdims=True), jnp.max(sn, axis=-1, keepdims=True))
    ec = jnp.exp2(sc - m)
    en = jnp.exp2(sn - m)
    den = jnp.sum(ec, axis=-1, keepdims=True) + jnp.sum(en, axis=-1, keepdims=True)
    o = (_dot(ec.astype(BF16), cv_ref[...].astype(BF16)) + _dot(en.astype(BF16), vn_ref[...].astype(BF16))) / den
    ohead = lax.broadcasted_iota(jnp.int32, (tn, D_ATT), 1) // HEAD_DIM
    out = jnp.zeros((tn, D_ATT), F32)
    for h in range(N_HEADS):
        out = out + jnp.where(ohead == h, o[h * tn:(h + 1) * tn], 0.0)
    o_ref[...] = out.astype(o_ref.dtype)


def _sample_attention(qs, kn, vn, floc, cache_k, cache_v, fc3, sks, sqs):
    nb, tn, _ = qs.shape
    past = cache_k.shape[1]
    per_b = lambda r, w: pl.BlockSpec((None, r, w), lambda b: (b, 0, 0))
    return pl.pallas_call(
        functools.partial(_sattn_kernel, tn=tn),
        grid=(nb,),
        in_specs=[per_b(tn, D_ATT), per_b(tn, D_ATT), per_b(tn, D_ATT), per_b(tn, LANES),
                  per_b(past, D_ATT), per_b(past, D_ATT), per_b(past, LANES),
                  _const_spec(sks.shape), _const_spec(sqs.shape)],
        out_specs=per_b(tn, D_ATT),
        out_shape=jax.ShapeDtypeStruct((nb, tn, D_ATT), BF16),
        compiler_params=_cparams(1),
        name="sample_attention",
    )(qs, kn, vn, floc, cache_k, cache_v, fc3, sks, sqs)


def _mixout_kernel(x_ref, o_ref, u_ref, halo_ref, cw_ref, cb_ref, lg_ref, lb_ref, wa_ref, wc_ref, gt_ref,
                   out_ref, ubuf_ref, ybuf_ref, *, tm, rc, zero_first):
    halo = halo_ref[...]
    if zero_first:
        halo = jnp.where(pl.program_id(1) > 0, halo, 0.0)
    ubuf_ref[0:HALO, :] = halo
    ubuf_ref[HALO:HALO + tm, :] = u_ref[...]
    lead = HALO - CONV_STATE
    for r0 in range(0, tm, rc):
        for c0 in range(0, D_CONV, LANES):
            acc = jnp.broadcast_to(cb_ref[:, c0:c0 + LANES], (rc, LANES))
            for sh in range(SUBLANES):
                taps = [j for j in range(CONV_WIDTH) if (lead + j) % SUBLANES == sh]
                span = max((lead + j) // SUBLANES for j in taps) * SUBLANES + rc
                ush = ubuf_ref[pl.ds(r0 + sh, span), pl.ds(c0, LANES)]
                for j in taps:
                    a = (lead + j) // SUBLANES * SUBLANES
                    acc = acc + cw_ref[j:j + 1, c0:c0 + LANES] * ush[a:a + rc]
            ybuf_ref[r0:r0 + rc, c0:c0 + LANES] = acc
    y = ybuf_ref[...]
    mu = jnp.mean(y, axis=-1, keepdims=True)
    yc = y - mu
    var = jnp.mean(yc * yc, axis=-1, keepdims=True)
    yn = yc * lax.rsqrt(var + EPS) * lg_ref[...] + lb_ref[...]
    ys = (yn * jax.nn.sigmoid(yn)).astype(BF16)
    m = _dot(o_ref[...], wa_ref[...]) + _dot(ys, wc_ref[...])
    out_ref[...] = x_ref[...] + gt_ref[...] * m


def _mixout(x, o, u, halo_arr, halo_map, cw, cb, lg, lb, wa, wc, gt, *, tm, zero_first):
    nb, t, _ = x.shape
    row = lambda w: pl.BlockSpec((None, tm, w), lambda b, i: (b, i, 0))
    rc = min(tm, 128)
    return pl.pallas_call(
        functools.partial(_mixout_kernel, tm=tm, rc=rc, zero_first=zero_first),
        grid=(nb, t // tm),
        in_specs=[row(D_MODEL), row(D_ATT), row(D_CONV),
                  pl.BlockSpec((None, HALO, D_CONV), halo_map),
                  _const_spec(cw.shape), _const_spec(cb.shape), _const_spec(lg.shape), _const_spec(lb.shape),
                  _const_spec(wa.shape), _const_spec(wc.shape), _mod_spec(gt, tm)],
        out_specs=row(D_MODEL),
        out_shape=jax.ShapeDtypeStruct(x.shape, F32),
        scratch_shapes=[pltpu.VMEM((HALO + tm, D_CONV), F32), pltpu.VMEM((tm, D_CONV), F32)],
        compiler_params=_cparams(2),
        name="mixout",
    )(x, o, u, halo_arr, cw, cb, lg, lb, wa, wc, gt)


def _selection_constants():
    hid = np.arange(D_ATT) // HEAD_DIM
    mblk = (hid[:, None] == hid[None, :]).astype(np.float32) / HEAD_DIM
    sk = np.zeros((LANES, N_PAIRS * LANES), np.float32)
    for g in range(N_PAIRS):
        base = g * LANES
        sk[ONE_LANE, base:base + F_PARTS] = 1.0
        for j in range(F_PARTS):
            sk[N_HEADS * j + 2 * g, base + F_PARTS + j] = -1.0
            sk[N_HEADS * j + 2 * g + 1, base + 2 * F_PARTS + j] = -1.0
    sq = np.zeros((N_HEADS, LANES, LANES), np.float32)
    for h in range(N_HEADS):
        for j in range(F_PARTS):
            sq[h, N_HEADS * j + h, j] = 1.0
            sq[h, ONE_LANE, F_PARTS + F_PARTS * (h % 2) + j] = 1.0
    sks = np.zeros((LANES, LANES), np.float32)
    sks[ONE_LANE, 0:F_PARTS] = 1.0
    sqs = np.zeros((N_HEADS, LANES, LANES), np.float32)
    for h in range(N_HEADS):
        for j in range(F_PARTS):
            sks[N_HEADS * j + h, F_PARTS + F_PARTS * h + j] = -1.0
            sqs[h, N_HEADS * j + h, j] = 1.0
            sqs[h, ONE_LANE, F_PARTS + F_PARTS * h + j] = 1.0
    r = np.arange(CUM_BLK)
    ltri = (r[None, :] <= r[:, None]).astype(np.float32)
    return mblk, sk, sq, sks, sqs, ltri


def _rep_lanes(a):
    rep = jnp.concatenate([a] * F_PARTS, axis=-1)
    pad = [(0, 0)] * (a.ndim - 1) + [(0, LANES - F_PARTS * N_HEADS)]
    return jnp.pad(rep, pad)


def kernel(x_prompt, x_sample, c_prompt, c_sample, cache_k, cache_v, cache_logf, state_conv, w_ada, b_ada, g_ffn1, w_up1, w_down1, g_mix, w_in, b_f, g_q, g_k, conv_w, conv_b, conv_ln_g, conv_ln_b, w_out, g_ffn2, w_up2, w_down2, g_final):
    depth = w_ada.shape[0]
    assert depth == 1, "single-layer trunk"
    nbp, seq, _ = x_prompt.shape
    nbs, tn, _ = x_sample.shape
    past = cache_k.shape[2]
    assert (nbs * tn) % CUM_BLK == 0 and CUM_BLK % tn == 0 and past % CUM_BLK == 0

    mblk_np, sk_np, sq_np, sks_np, sqs_np, ltri_np = _selection_constants()
    mblk, sk, sq, sks, sqs, ltri = (jnp.asarray(a, BF16) for a in (mblk_np, sk_np, sq_np, sks_np, sqs_np, ltri_np))
    r = np.arange(CUM_BLK)
    ltri_s = jnp.asarray(ltri_np * (r[:, None] // tn == r[None, :] // tn), BF16)

    l = 0
    row = lambda a: a[l].reshape(1, -1)
    w_up1b, w_dn1b = w_up1[l].astype(BF16), w_down1[l].astype(BF16)
    w_up2b, w_dn2b = w_up2[l].astype(BF16), w_down2[l].astype(BF16)
    w_qkv = w_in[l][:, :3 * D_ATT].astype(BF16)
    w_f = _rep_lanes(w_in[l][:, 3 * D_ATT:3 * D_ATT + N_HEADS]).astype(BF16)
    w_u = w_in[l][:, 3 * D_ATT + N_HEADS:].astype(BF16)
    b_f3 = _rep_lanes(b_f[l].reshape(1, -1))
    gq = jnp.tile(g_q[l], N_HEADS).reshape(1, -1)
    gk = jnp.tile(g_k[l], N_HEADS).reshape(1, -1)
    w_oa, w_oc = w_out[l][:D_ATT].astype(BF16), w_out[l][D_ATT:].astype(BF16)
    cw = jnp.pad(conv_w[l], ((0, 1), (0, 0)))

    mod = _adaln(jnp.concatenate([c_prompt, c_sample], axis=0), w_ada[l], b_ada[l])
    mods = [mod[:, i * D_MODEL:(i + 1) * D_MODEL] for i in range(N_MOD)]
    mp = [m[:nbp].reshape(nbp, 1, D_MODEL) for m in mods]
    ms_tok = [jnp.repeat(m[nbp:], tn, axis=0).reshape(1, nbs * tn, D_MODEL) for m in mods]
    ms_b = [m[nbp:].reshape(nbs, 1, D_MODEL) for m in mods]

    tm = 512
    xp = _ffn(x_prompt, row(g_ffn1), mp[0], mp[1], mp[2], w_up1b, w_dn1b, row(g_final), tm=tm, final=False)
    qs, kp, vp, fp, up, kx, vx, pk = _mixin(xp, row(g_mix), mp[3], mp[4], w_qkv, w_f, w_u, b_f3, gq, gk,
                                            mblk, ltri, sk, tm=tm, prompt=True)
    op = _attention(qs, pk, sq, kx, vx, tq=tm)
    halo_blocks = tm // HALO
    xp = _mixout(xp, op, up, up, lambda b, i: (b, jnp.maximum(i * halo_blocks - 1, 0), 0),
                 cw, row(conv_b), row(conv_ln_g), row(conv_ln_b), w_oa, w_oc, mp[5], tm=tm, zero_first=True)
    yp = _ffn(xp, row(g_ffn2), mp[6], mp[7], mp[8], w_up2b, w_dn2b, row(g_final), tm=tm, final=True)

    ns = nbs * tn
    xs = x_sample.reshape(1, ns, D_MODEL)
    xs = _ffn(xs, row(g_ffn1), ms_tok[0], ms_tok[1], ms_tok[2], w_up1b, w_dn1b, row(g_final), tm=ns, final=False)
    qss, ks, vs, fs, us, floc = _mixin(xs, row(g_mix), ms_tok[3], ms_tok[4], w_qkv, w_f, w_u, b_f3, gq, gk,
                                       mblk, ltri_s, sk, tm=ns, prompt=False)
    fc = _cumsum(cache_logf[l].transpose(1, 0, 2).reshape(past, nbs * N_HEADS), ltri)
    fc3 = _rep_lanes(fc.reshape(past, nbs, N_HEADS).transpose(1, 0, 2))
    per_b = lambda a: a.reshape(nbs, tn, a.shape[-1])
    os_ = _sample_attention(per_b(qss), per_b(ks), per_b(vs), per_b(floc), cache_k[l].reshape(nbs, past, D_ATT),
                            cache_v[l].reshape(nbs, past, D_ATT), fc3, sks, sqs)
    state = jnp.pad(state_conv[l], ((0, 0), (HALO - CONV_STATE, 0), (0, 0)))
    xs = _mixout(per_b(xs), os_, per_b(us), state, lambda b, i: (b, 0, 0),
                 cw, row(conv_b), row(conv_ln_g), row(conv_ln_b), w_oa, w_oc, ms_b[5], tm=tn, zero_first=False)
    ys = _ffn(xs.reshape(1, ns, D_MODEL), row(g_ffn2), ms_tok[6], ms_tok[7], ms_tok[8], w_up2b, w_dn2b,
              row(g_final), tm=ns, final=True)

    heads = lambda a, nb, t: a.reshape(1, nb, t, N_HEADS, HEAD_DIM)
    conv_p = up[:, seq - CONV_STATE:][None]
    conv_s = jnp.concatenate([state_conv[l], per_b(us)], axis=1)[:, tn:][None]
    return (yp, ys.reshape(nbs, tn, D_MODEL),
            heads(kp, nbp, seq), heads(vp, nbp, seq), fp[None], conv_p,
            heads(ks, nbs, tn), heads(vs, nbs, tn), per_b(fs)[None], conv_s)
```

```python
import functools

import jax
import jax.numpy as jnp
import numpy as np
from jax import lax
from jax.experimental import pallas as pl
from jax.experimental.pallas import tpu as pltpu

F32 = jnp.float32
BF16 = jnp.bfloat16

D_MODEL = 1024
D_ATT = 512
HEAD_DIM = 64
N_HEADS = 8
N_PAIRS = N_HEADS // 2
D_CONV = 512
CONV_WIDTH = 31
CONV_STATE = CONV_WIDTH - 1
D_FF = 2816
N_MOD = 9
EPS = 1e-6
LOG2E = 1.4426950408889634
NEG_BIG = -1e30

LANES = 128
SUBLANES = 8
HALO = 32
FF_CHUNK = 256
CUM_BLK = 128
F_PARTS = 3
ONE_LANE = F_PARTS * N_HEADS
VMEM_LIMIT = 56 * 1024 * 1024


def _cparams(n_axes):
    return pltpu.CompilerParams(dimension_semantics=("arbitrary",) * n_axes,
                                vmem_limit_bytes=VMEM_LIMIT)


def _const_spec(shape):
    nd = len(shape)
    return pl.BlockSpec(shape, lambda *_: (0,) * nd, pipeline_mode=pl.Buffered(1))


def _dot(a, b):
    return jnp.dot(a, b, preferred_element_type=F32)


def _dot_nt(a, b):
    return lax.dot_general(a, b, (((1,), (1,)), ((), ())), preferred_element_type=F32)


def _rms(x, g):
    return x * lax.rsqrt(jnp.mean(x * x, axis=-1, keepdims=True) + EPS) * g


def _split3(x):
    p1 = x.astype(BF16)
    r1 = x - p1.astype(F32)
    p2 = r1.astype(BF16)
    p3 = (r1 - p2.astype(F32)).astype(BF16)
    return p1, p2, p3


def _pack_forget(f3):
    p1, p2, p3 = _split3(f3)
    lane = lax.broadcasted_iota(jnp.int32, f3.shape, 1)
    one = jnp.where(lane == ONE_LANE, 1.0, 0.0).astype(BF16)
    return jnp.where(lane < N_HEADS, p1,
                     jnp.where(lane < 2 * N_HEADS, p2,
                               jnp.where(lane < 3 * N_HEADS, p3, one)))


def _cumsum_rows(lf, ltri, carry):
    outs = []
    for sb in range(lf.shape[0] // CUM_BLK):
        p1, p2, p3 = _split3(lf[sb * CUM_BLK:(sb + 1) * CUM_BLK])
        c = _dot(ltri, p1) + _dot(ltri, p2) + _dot(ltri, p3)
        if carry is not None:
            c = c + carry
            carry = c[CUM_BLK - 1:CUM_BLK, :]
        outs.append(c)
    return jnp.concatenate(outs, axis=0), carry


def _ada_kernel(c_ref, w_ref, b_ref, o_ref):
    c = c_ref[...]
    s = (c * jax.nn.sigmoid(c)).astype(BF16)
    o_ref[...] = _dot(s, w_ref[...].astype(BF16)) + b_ref[...]


def _adaln(c_all, w_ada, b_ada):
    n = c_all.shape[0]
    tn = D_MODEL
    return pl.pallas_call(
        _ada_kernel,
        grid=(N_MOD * D_MODEL // tn,),
        in_specs=[pl.BlockSpec((n, D_MODEL), lambda j: (0, 0)),
                  pl.BlockSpec((D_MODEL, tn), lambda j: (0, j)),
                  pl.BlockSpec((1, tn), lambda j: (0, j))],
        out_specs=pl.BlockSpec((n, tn), lambda j: (0, j)),
        out_shape=jax.ShapeDtypeStruct((n, N_MOD * D_MODEL), F32),
        compiler_params=_cparams(1),
        name="adaln",
    )(c_all, w_ada, b_ada.reshape(1, -1))


def _mod_spec(arr, tm):
    if arr.shape[1] == 1:
        return pl.BlockSpec((None, 1, D_MODEL), lambda b, t: (b, 0, 0))
    return pl.BlockSpec((None, tm, D_MODEL), lambda b, t: (b, t, 0))


def _ffn_kernel(x_ref, g_ref, sh_ref, sc_ref, gt_ref, wup_ref, wdn_ref, gfin_ref, o_ref, gs_ref, *, final):
    x = x_ref[...]
    h = (_rms(x, g_ref[...]) * (1.0 + sc_ref[...]) + sh_ref[...]).astype(BF16)
    for c in range(D_FF // FF_CHUNK):
        lo = c * FF_CHUNK
        a = _dot(h, wup_ref[:, lo:lo + FF_CHUNK])
        b = _dot(h, wup_ref[:, D_FF + lo:D_FF + lo + FF_CHUNK])
        gs_ref[:, lo:lo + FF_CHUNK] = (a * jax.nn.sigmoid(a) * b).astype(BF16)
    y = _dot(gs_ref[...], wdn_ref[...])
    out = x + 0.5 * gt_ref[...] * y
    if final:
        out = _rms(out, gfin_ref[...])
    o_ref[...] = out


def _ffn(x, g, sh, sc, gt, w_up, w_dn, g_final, *, tm, final):
    nb, t, _ = x.shape
    xspec = pl.BlockSpec((None, tm, D_MODEL), lambda b, i: (b, i, 0))
    return pl.pallas_call(
        functools.partial(_ffn_kernel, final=final),
        grid=(nb, t // tm),
        in_specs=[xspec, _const_spec((1, D_MODEL)), _mod_spec(sh, tm), _mod_spec(sc, tm), _mod_spec(gt, tm),
                  _const_spec(w_up.shape), _const_spec(w_dn.shape), _const_spec((1, D_MODEL))],
        out_specs=xspec,
        out_shape=jax.ShapeDtypeStruct(x.shape, F32),
        scratch_shapes=[pltpu.VMEM((tm, D_FF), BF16)],
        compiler_params=_cparams(2),
        name="ffn_final" if final else "ffn",
    )(x, g, sh, sc, gt, w_up, w_dn, g_final)


def _mixin_kernel(x_ref, g_ref, sh_ref, sc_ref, wqkv_ref, wf_ref, wu_ref, bf_ref, gq_ref, gk_ref,
                  mblk_ref, ltri_ref, sk_ref,
                  qs_ref, k_ref, v_ref, logf_ref, u_ref, *rest, prompt):
    if prompt:
        kx_ref, vx_ref, p_ref, carry_ref = rest
    else:
        (floc_ref,) = rest
    x = x_ref[...]
    h = (_rms(x, g_ref[...]) * (1.0 + sc_ref[...]) + sh_ref[...]).astype(BF16)

    qkv = _dot(h, wqkv_ref[...])
    q = qkv[:, 0:D_ATT]
    k = qkv[:, D_ATT:2 * D_ATT]
    v = qkv[:, 2 * D_ATT:3 * D_ATT]
    mblk = mblk_ref[...]
    qn = q * lax.rsqrt(_dot((q * q).astype(BF16), mblk) + EPS) * gq_ref[...]
    kn = k * lax.rsqrt(_dot((k * k).astype(BF16), mblk) + EPS) * gk_ref[...]
    qs_ref[...] = (qn * (LOG2E * HEAD_DIM ** -0.5)).astype(BF16)
    k_ref[...] = kn
    v_ref[...] = v

    au = _dot(h, wu_ref[...])
    u_ref[...] = au[:, 0:D_CONV] * jax.nn.sigmoid(au[:, D_CONV:2 * D_CONV])

    lf = jax.nn.log_sigmoid(_dot(h, wf_ref[...]) + bf_ref[...])
    logf_ref[...] = lf[:, 0:N_HEADS]

    if prompt:
        @pl.when(pl.program_id(1) == 0)
        def _():
            carry_ref[...] = jnp.zeros_like(carry_ref)
        fcum, carry = _cumsum_rows(lf, ltri_ref[...], carry_ref[0:1, :])
        carry_ref[0:1, :] = carry
        p = _pack_forget(fcum * LOG2E)
        p_ref[...] = p
        e = _dot(p, sk_ref[...]).astype(BF16)
        for g in range(N_PAIRS):
            sl = slice(g * LANES, (g + 1) * LANES)
            kx_ref[g, :, 0:LANES] = kn[:, sl].astype(BF16)
            kx_ref[g, :, LANES:2 * LANES] = e[:, sl]
        vt = v.T.astype(BF16)
        ones = jnp.ones((HEAD_DIM, x.shape[0]), BF16)
        for h in range(N_HEADS):
            vx_ref[h, 0:HEAD_DIM, :] = vt[h * HEAD_DIM:(h + 1) * HEAD_DIM]
            vx_ref[h, HEAD_DIM:2 * HEAD_DIM, :] = ones
    else:
        fcum, _ = _cumsum_rows(lf, ltri_ref[...], None)
        floc_ref[...] = fcum


def _mixin(x, g, sh, sc, w_qkv, w_f, w_u, b_f, gq, gk, mblk, ltri, sk, *, tm, prompt):
    nb, t, _ = x.shape
    row = lambda w: pl.BlockSpec((None, tm, w), lambda b, i: (b, i, 0))
    out_shape = [jax.ShapeDtypeStruct((nb, t, D_ATT), BF16),
                 jax.ShapeDtypeStruct((nb, t, D_ATT), F32),
                 jax.ShapeDtypeStruct((nb, t, D_ATT), F32),
                 jax.ShapeDtypeStruct((nb, t, N_HEADS), F32),
                 jax.ShapeDtypeStruct((nb, t, D_CONV), F32)]
    out_specs = [row(D_ATT), row(D_ATT), row(D_ATT), row(N_HEADS), row(D_CONV)]
    scratch = []
    if prompt:
        pair = pl.BlockSpec((None, N_PAIRS, tm, 2 * LANES), lambda b, i: (b, 0, i, 0))
        vspec = pl.BlockSpec((None, N_HEADS, None, 2 * HEAD_DIM, tm), lambda b, i: (b, 0, i, 0, 0))
        out_shape += [jax.ShapeDtypeStruct((nb, N_PAIRS, t, 2 * LANES), BF16),
                      jax.ShapeDtypeStruct((nb, N_HEADS, t // tm, 2 * HEAD_DIM, tm), BF16),
                      jax.ShapeDtypeStruct((nb, t, LANES), BF16)]
        out_specs += [pair, vspec, row(LANES)]
        scratch = [pltpu.VMEM((SUBLANES, LANES), F32)]
    else:
        out_shape += [jax.ShapeDtypeStruct((nb, t, LANES), F32)]
        out_specs += [row(LANES)]
    return pl.pallas_call(
        functools.partial(_mixin_kernel, prompt=prompt),
        grid=(nb, t // tm),
        in_specs=[row(D_MODEL), _const_spec((1, D_MODEL)), _mod_spec(sh, tm), _mod_spec(sc, tm),
                  _const_spec(w_qkv.shape), _const_spec(w_f.shape), _const_spec(w_u.shape),
                  _const_spec(b_f.shape), _const_spec(gq.shape), _const_spec(gk.shape),
                  _const_spec(mblk.shape), _const_spec(ltri.shape), _const_spec(sk.shape)],
        out_specs=out_specs,
        out_shape=out_shape,
        scratch_shapes=scratch,
        compiler_params=_cparams(2),
        name="mixin_prompt" if prompt else "mixin_sample",
    )(x, g, sh, sc, w_qkv, w_f, w_u, b_f, gq, gk, mblk, ltri, sk)


def _attn_kernel(q_ref, p_ref, sq_ref, kx_ref, vx_ref, o_ref, qx_ref, s0_ref, s1_ref, m_ref, acc_ref, *, tq):
    i = pl.program_id(2)
    first = lax.broadcasted_iota(jnp.int32, (tq, LANES), 1) < HEAD_DIM
    qp = q_ref[...].astype(F32)
    p = p_ref[...]
    for hd in range(2):
        qm = jnp.where(first if hd == 0 else jnp.logical_not(first), qp, 0.0)
        qx_ref[hd] = jnp.concatenate([qm, _dot(p, sq_ref[hd])], axis=1).T.astype(BF16)
    m_ref[...] = jnp.full_like(m_ref, NEG_BIG)
    acc_ref[...] = jnp.zeros_like(acc_ref)
    s_refs = (s0_ref, s1_ref)

    def scores(j, slot):
        kx = kx_ref[pl.ds(pl.multiple_of(j * tq, tq), tq), :]
        for hd in range(2):
            s_refs[slot][hd] = _dot(kx, qx_ref[hd])

    def update(j, slot, masked):
        for hd in range(2):
            s = s_refs[slot][hd]
            if masked:
                key = lax.broadcasted_iota(jnp.int32, s.shape, 0)
                qry = lax.broadcasted_iota(jnp.int32, s.shape, 1)
                s = jnp.where(key <= qry, s, NEG_BIG)
            m_old = m_ref[hd]
            m_new = jnp.maximum(m_old, jnp.max(s, axis=0, keepdims=True))
            pr = jnp.exp2(s - m_new).astype(BF16)
            acc_ref[hd] = acc_ref[hd] * jnp.exp2(m_old - m_new) + _dot(vx_ref[hd, j], pr)
            m_ref[hd] = m_new

    scores(0, 0)

    def body(jj, carry):
        j = 2 * jj
        scores(j + 1, 1)
        update(j, 0, False)
        scores(j + 2, 0)
        update(j + 1, 1, False)
        return carry

    lax.fori_loop(0, i // 2, body, 0)

    @pl.when(i % 2 == 0)
    def _():
        update(i, 0, True)

    @pl.when(i % 2 == 1)
    def _():
        scores(i, 1)
        update(i - 1, 0, False)
        update(i, 1, True)

    out_t = jnp.concatenate([acc_ref[hd, 0:HEAD_DIM, :] / acc_ref[hd, HEAD_DIM:HEAD_DIM + 1, :] for hd in range(2)],
                            axis=0)
    o_ref[...] = out_t.T.astype(o_ref.dtype)


def _attention(qs, p, sq, kx, vx, *, tq):
    nb, t, _ = qs.shape
    nk = t // tq
    blk = lambda b, g, i: (b, i, g)
    return pl.pallas_call(
        functools.partial(_attn_kernel, tq=tq),
        grid=(nb, N_PAIRS, nk),
        in_specs=[pl.BlockSpec((None, tq, LANES), blk),
                  pl.BlockSpec((None, tq, LANES), lambda b, g, i: (b, i, 0)),
                  pl.BlockSpec((2, LANES, LANES), lambda b, g, i: (g, 0, 0)),
                  pl.BlockSpec((None, None, t, 2 * LANES), lambda b, g, i: (b, g, 0, 0)),
                  pl.BlockSpec((None, 2, nk, 2 * HEAD_DIM, tq), lambda b, g, i: (b, g, 0, 0, 0))],
        out_specs=pl.BlockSpec((None, tq, LANES), blk),
        out_shape=jax.ShapeDtypeStruct((nb, t, D_ATT), BF16),
        scratch_shapes=[pltpu.VMEM((2, 2 * LANES, tq), BF16),
                        pltpu.VMEM((2, tq, tq), F32), pltpu.VMEM((2, tq, tq), F32),
                        pltpu.VMEM((2, 1, tq), F32), pltpu.VMEM((2, 2 * HEAD_DIM, tq), F32)],
        compiler_params=_cparams(3),
        name="fox_attention",
    )(qs, p, sq, kx, vx)


def _cumsum_kernel(x_ref, ltri_ref, o_ref):
    out, _ = _cumsum_rows(x_ref[...], ltri_ref[...], jnp.zeros((1, LANES), F32))
    o_ref[...] = out


def _cumsum(x, ltri):
    return pl.pallas_call(
        _cumsum_kernel,
        out_shape=jax.ShapeDtypeStruct(x.shape, F32),
        compiler_params=pltpu.CompilerParams(vmem_limit_bytes=VMEM_LIMIT),
        name="cache_cumsum",
    )(x, ltri)


def _sattn_kernel(q_ref, kn_ref, vn_ref, fl_ref, ck_ref, cv_ref, fc_ref, sks_ref, sqs_ref, o_ref, *, tn):
    fc = fc_ref[...]
    fn = fl_ref[...] + fc[fc.shape[0] - 1:, :]
    pc = _pack_forget(fc * LOG2E)
    pn = _pack_forget(fn * LOG2E)
    sks = sks_ref[...]
    kc = jnp.concatenate([ck_ref[...].astype(BF16), _dot(pc, sks).astype(BF16)], axis=1)
    kn = jnp.concatenate([kn_ref[...].astype(BF16), _dot(pn, sks).astype(BF16)], axis=1)
    q = q_ref[...]
    head = lax.broadcasted_iota(jnp.int32, q.shape, 1) // HEAD_DIM
    zero = jnp.zeros_like(q)
    qm = jnp.concatenate(
        [jnp.concatenate([jnp.where(head == h, q, zero), _dot(pn, sqs_ref[h]).astype(BF16)], axis=1)
         for h in range(N_HEADS)], axis=0)
    sc = _dot_nt(qm, kc)
    sn = _dot_nt(qm, kn)
    r = lax.broadcasted_iota(jnp.int32, sn.shape, 0) % tn
    c = lax.broadcasted_iota(jnp.int32, sn.shape, 1)
    sn = jnp.where(c <= r, sn, NEG_BIG)
    m = jnp.maximum(jnp.max(sc, axis=-1, keepdims=True), jnp.max(sn, axis=-1, keepdims=True))
    ec = jnp.exp2(sc - m)
    en = jnp.exp2(sn - m)
    den = jnp.sum(ec, axis=-1, keepdims=True) + jnp.sum(en, axis=-1, keepdims=True)
    o = (_dot(ec.astype(BF16), cv_ref[...].astype(BF16)) + _dot(en.astype(BF16), vn_ref[...].astype(BF16))) / den
    ohead = lax.broadcasted_iota(jnp.int32, (tn, D_ATT), 1) // HEAD_DIM
    out = jnp.zeros((tn, D_ATT), F32)
    for h in range(N_HEADS):
        out = out + jnp.where(ohead == h, o[h * tn:(h + 1) * tn], 0.0)
    o_ref[...] = out.astype(o_ref.dtype)


def _sample_attention(qs, kn, vn, floc, cache_k, cache_v, fc3, sks, sqs):
    nb, tn, _ = qs.shape
    past = cache_k.shape[1]
    per_b = lambda r, w: pl.BlockSpec((None, r, w), lambda b: (b, 0, 0))
    return pl.pallas_call(
        functools.partial(_sattn_kernel, tn=tn),
        grid=(nb,),
        in_specs=[per_b(tn, D_ATT), per_b(tn, D_ATT), per_b(tn, D_ATT), per_b(tn, LANES),
                  per_b(past, D_ATT), per_b(past, D_ATT), per_b(past, LANES),
                  _const_spec(sks.shape), _const_spec(sqs.shape)],
        out_specs=per_b(tn, D_ATT),
        out_shape=jax.ShapeDtypeStruct((nb, tn, D_ATT), BF16),
        compiler_params=_cparams(1),
        name="sample_attention",
    )(qs, kn, vn, floc, cache_k, cache_v, fc3, sks, sqs)


def _mixout_kernel(x_ref, o_ref, u_ref, halo_ref, cw_ref, cb_ref, lg_ref, lb_ref, wa_ref, wc_ref, gt_ref,
                   out_ref, ubuf_ref, ybuf_ref, *, tm, rc, zero_first):
    halo = halo_ref[...]
    if zero_first:
        halo = jnp.where(pl.program_id(1) > 0, halo, 0.0)
    ubuf_ref[0:HALO, :] = halo
    ubuf_ref[HALO:HALO + tm, :] = u_ref[...]
    lead = HALO - CONV_STATE
    for r0 in range(0, tm, rc):
        for c0 in range(0, D_CONV, LANES):
            acc = jnp.broadcast_to(cb_ref[:, c0:c0 + LANES], (rc, LANES))
            for sh in range(SUBLANES):
                taps = [j for j in range(CONV_WIDTH) if (lead + j) % SUBLANES == sh]
                span = max((lead + j) // SUBLANES for j in taps) * SUBLANES + rc
                ush = ubuf_ref[pl.ds(r0 + sh, span), pl.ds(c0, LANES)]
                for j in taps:
                    a = (lead + j) // SUBLANES * SUBLANES
                    acc = acc + cw_ref[j:j + 1, c0:c0 + LANES] * ush[a:a + rc]
            ybuf_ref[r0:r0 + rc, c0:c0 + LANES] = acc
    y = ybuf_ref[...]
    mu = jnp.mean(y, axis=-1, keepdims=True)
    yc = y - mu
    var = jnp.mean(yc * yc, axis=-1, keepdims=True)
    yn = yc * lax.rsqrt(var + EPS) * lg_ref[...] + lb_ref[...]
    ys = (yn * jax.nn.sigmoid(yn)).astype(BF16)
    m = _dot(o_ref[...], wa_ref[...]) + _dot(ys, wc_ref[...])
    out_ref[...] = x_ref[...] + gt_ref[...] * m


def _mixout(x, o, u, halo_arr, halo_map, cw, cb, lg, lb, wa, wc, gt, *, tm, zero_first):
    nb, t, _ = x.shape
    row = lambda w: pl.BlockSpec((None, tm, w), lambda b, i: (b, i, 0))
    rc = min(tm, 128)
    return pl.pallas_call(
        functools.partial(_mixout_kernel, tm=tm, rc=rc, zero_first=zero_first),
        grid=(nb, t // tm),
        in_specs=[row(D_MODEL), row(D_ATT), row(D_CONV),
                  pl.BlockSpec((None, HALO, D_CONV), halo_map),
                  _const_spec(cw.shape), _const_spec(cb.shape), _const_spec(lg.shape), _const_spec(lb.shape),
                  _const_spec(wa.shape), _const_spec(wc.shape), _mod_spec(gt, tm)],
        out_specs=row(D_MODEL),
        out_shape=jax.ShapeDtypeStruct(x.shape, F32),
        scratch_shapes=[pltpu.VMEM((HALO + tm, D_CONV), F32), pltpu.VMEM((tm, D_CONV), F32)],
        compiler_params=_cparams(2),
        name="mixout",
    )(x, o, u, halo_arr, cw, cb, lg, lb, wa, wc, gt)


def _selection_constants():
    hid = np.arange(D_ATT) // HEAD_DIM
    mblk = (hid[:, None] == hid[None, :]).astype(np.float32) / HEAD_DIM
    sk = np.zeros((LANES, N_PAIRS * LANES), np.float32)
    for g in range(N_PAIRS):
        base = g * LANES
        sk[ONE_LANE, base:base + F_PARTS] = 1.0
        for j in range(F_PARTS):
            sk[N_HEADS * j + 2 * g, base + F_PARTS + j] = -1.0
            sk[N_HEADS * j + 2 * g + 1, base + 2 * F_PARTS + j] = -1.0
    sq = np.zeros((N_HEADS, LANES, LANES), np.float32)
    for h in range(N_HEADS):
        for j in range(F_PARTS):
            sq[h, N_HEADS * j + h, j] = 1.0
            sq[h, ONE_LANE, F_PARTS + F_PARTS * (h % 2) + j] = 1.0
    sks = np.zeros((LANES, LANES), np.float32)
    sks[ONE_LANE, 0:F_PARTS] = 1.0
    sqs = np.zeros((N_HEADS, LANES, LANES), np.float32)
    for h in range(N_HEADS):
        for j in range(F_PARTS):
            sks[N_HEADS * j + h, F_PARTS + F_PARTS * h + j] = -1.0
            sqs[h, N_HEADS * j + h, j] = 1.0
            sqs[h, ONE_LANE, F_PARTS + F_PARTS * h + j] = 1.0
    r = np.arange(CUM_BLK)
    ltri = (r[None, :] <= r[:, None]).astype(np.float32)
    return mblk, sk, sq, sks, sqs, ltri


def _rep_lanes(a):
    rep = jnp.concatenate([a] * F_PARTS, axis=-1)
    pad = [(0, 0)] * (a.ndim - 1) + [(0, LANES - F_PARTS * N_HEADS)]
    return jnp.pad(rep, pad)


def kernel(x_prompt, x_sample, c_prompt, c_sample, cache_k, cache_v, cache_logf, state_conv, w_ada, b_ada, g_ffn1, w_up1, w_down1, g_mix, w_in, b_f, g_q, g_k, conv_w, conv_b, conv_ln_g, conv_ln_b, w_out, g_ffn2, w_up2, w_down2, g_final):
    depth = w_ada.shape[0]
    assert depth == 1, "single-layer trunk"
    nbp, seq, _ = x_prompt.shape
    nbs, tn, _ = x_sample.shape
    past = cache_k.shape[2]
    assert (nbs * tn) % CUM_BLK == 0 and CUM_BLK % tn == 0 and past % CUM_BLK == 0

    mblk_np, sk_np, sq_np, sks_np, sqs_np, ltri_np = _selection_constants()
    mblk, sk, sq, sks, sqs, ltri = (jnp.asarray(a, BF16) for a in (mblk_np, sk_np, sq_np, sks_np, sqs_np, ltri_np))
    r = np.arange(CUM_BLK)
    ltri_s = jnp.asarray(ltri_np * (r[:, None] // tn == r[None, :] // tn), BF16)

    l = 0
    row = lambda a: a[l].reshape(1, -1)
    w_up1b, w_dn1b = w_up1[l].astype(BF16), w_down1[l].astype(BF16)
    w_up2b, w_dn2b = w_up2[l].astype(BF16), w_down2[l].astype(BF16)
    w_qkv = w_in[l][:, :3 * D_ATT].astype(BF16)
    w_f = _rep_lanes(w_in[l][:, 3 * D_ATT:3 * D_ATT + N_HEADS]).astype(BF16)
    w_u = w_in[l][:, 3 * D_ATT + N_HEADS:].astype(BF16)
    b_f3 = _rep_lanes(b_f[l].reshape(1, -1))
    gq = jnp.tile(g_q[l], N_HEADS).reshape(1, -1)
    gk = jnp.tile(g_k[l], N_HEADS).reshape(1, -1)
    w_oa, w_oc = w_out[l][:D_ATT].astype(BF16), w_out[l][D_ATT:].astype(BF16)
    cw = jnp.pad(conv_w[l], ((0, 1), (0, 0)))

    mod = _adaln(jnp.concatenate([c_prompt, c_sample], axis=0), w_ada[l], b_ada[l])
    mods = [mod[:, i * D_MODEL:(i + 1) * D_MODEL] for i in range(N_MOD)]
    mp = [m[:nbp].reshape(nbp, 1, D_MODEL) for m in mods]
    ms_tok = [jnp.repeat(m[nbp:], tn, axis=0).reshape(1, nbs * tn, D_MODEL) for m in mods]
    ms_b = [m[nbp:].reshape(nbs, 1, D_MODEL) for m in mods]

    tm = 512
    xp = _ffn(x_prompt, row(g_ffn1), mp[0], mp[1], mp[2], w_up1b, w_dn1b, row(g_final), tm=tm, final=False)
    qs, kp, vp, fp, up, kx, vx, pk = _mixin(xp, row(g_mix), mp[3], mp[4], w_qkv, w_f, w_u, b_f3, gq, gk,
                                            mblk, ltri, sk, tm=tm, prompt=True)
    op = _attention(qs, pk, sq, kx, vx, tq=tm)
    halo_blocks = tm // HALO
    xp = _mixout(xp, op, up, up, lambda b, i: (b, jnp.maximum(i * halo_blocks - 1, 0), 0),
                 cw, row(conv_b), row(conv_ln_g), row(conv_ln_b), w_oa, w_oc, mp[5], tm=tm, zero_first=True)
    yp = _ffn(xp, row(g_ffn2), mp[6], mp[7], mp[8], w_up2b, w_dn2b, row(g_final), tm=tm, final=True)

    ns = nbs * tn
    xs = x_sample.reshape(1, ns, D_MODEL)
    xs = _ffn(xs, row(g_ffn1), ms_tok[0], ms_tok[1], ms_tok[2], w_up1b, w_dn1b, row(g_final), tm=ns, final=False)
    qss, ks, vs, fs, us, floc = _mixin(xs, row(g_mix), ms_tok[3], ms_tok[4], w_qkv, w_f, w_u, b_f3, gq, gk,
                                       mblk, ltri_s, sk, tm=ns, prompt=False)
    fc = _cumsum(cache_logf[l].transpose(1, 0, 2).reshape(past, nbs * N_HEADS), ltri)
    fc3 = _rep_lanes(fc.reshape(past, nbs, N_HEADS).transpose(1, 0, 2))
    per_b = lambda a: a.reshape(nbs, tn, a.shape[-1])
    os_ = _sample_attention(per_b(qss), per_b(ks), per_b(vs), per_b(floc), cache_k[l].reshape(nbs, past, D_ATT),
                            cache_v[l].reshape(nbs, past, D_ATT), fc3, sks, sqs)
    state = jnp.pad(state_conv[l], ((0, 0), (HALO - CONV_STATE, 0), (0, 0)))
    xs = _mixout(per_b(xs), os_, per_b(us), state, lambda b, i: (b, 0, 0),
                 cw, row(conv_b), row(conv_ln_g), row(conv_ln_b), w_oa, w_oc, ms_b[5], tm=tn, zero_first=False)
    ys = _ffn(xs.reshape(1, ns, D_MODEL), row(g_ffn2), ms_tok[6], ms_tok[7], ms_tok[8], w_up2b, w_dn2b,
              row(g_final), tm=ns, final=True)

    heads = lambda a, nb, t: a.reshape(1, nb, t, N_HEADS, HEAD_DIM)
    conv_p = up[:, seq - CONV_STATE:][None]
    conv_s = jnp.concatenate([state_conv[l], per_b(us)], axis=1)[:, tn:][None]
    return (yp, ys.reshape(nbs, tn, D_MODEL),
            heads(kp, nbp, seq), heads(vp, nbp, seq), fp[None], conv_p,
            heads(ks, nbs, tn), heads(vs, nbs, tn), per_b(fs)[None], conv_s)
```

```python
import functools

import jax
import jax.numpy as jnp
import numpy as np
from jax import lax
from jax.experimental import pallas as pl
from jax.experimental.pallas import tpu as pltpu

F32 = jnp.float32
BF16 = jnp.bfloat16

D_MODEL = 1024
D_ATT = 512
HEAD_DIM = 64
N_HEADS = 8
N_PAIRS = N_HEADS // 2
D_CONV = 512
CONV_WIDTH = 31
CONV_STATE = CONV_WIDTH - 1
D_FF = 2816
N_MOD = 9
EPS = 1e-6
LOG2E = 1.4426950408889634
NEG_BIG = -1e30

LANES = 128
SUBLANES = 8
HALO = 32
FF_CHUNK = 256
CUM_BLK = 128
F_PARTS = 3
ONE_LANE = F_PARTS * N_HEADS
VMEM_LIMIT = 56 * 1024 * 1024


def _cparams(n_axes):
    return pltpu.CompilerParams(dimension_semantics=("arbitrary",) * n_axes,
                                vmem_limit_bytes=VMEM_LIMIT)


def _const_spec(shape):
    nd = len(shape)
    return pl.BlockSpec(shape, lambda *_: (0,) * nd, pipeline_mode=pl.Buffered(1))


def _dot(a, b):
    return jnp.dot(a, b, preferred_element_type=F32)


def _dot_nt(a, b):
    return lax.dot_general(a, b, (((1,), (1,)), ((), ())), preferred_element_type=F32)


def _rms(x, g):
    return x * lax.rsqrt(jnp.mean(x * x, axis=-1, keepdims=True) + EPS) * g


def _split3(x):
    p1 = x.astype(BF16)
    r1 = x - p1.astype(F32)
    p2 = r1.astype(BF16)
    p3 = (r1 - p2.astype(F32)).astype(BF16)
    return p1, p2, p3


def _pack_forget(f3):
    p1, p2, p3 = _split3(f3)
    lane = lax.broadcasted_iota(jnp.int32, f3.shape, 1)
    one = jnp.where(lane == ONE_LANE, 1.0, 0.0).astype(BF16)
    return jnp.where(lane < N_HEADS, p1,
                     jnp.where(lane < 2 * N_HEADS, p2,
                               jnp.where(lane < 3 * N_HEADS, p3, one)))


def _cumsum_rows(lf, ltri, carry):
    outs = []
    for sb in range(lf.shape[0] // CUM_BLK):
        p1, p2, p3 = _split3(lf[sb * CUM_BLK:(sb + 1) * CUM_BLK])
        c = _dot(ltri, p1) + _dot(ltri, p2) + _dot(ltri, p3)
        if carry is not None:
            c = c + carry
            carry = c[CUM_BLK - 1:CUM_BLK, :]
        outs.append(c)
    return jnp.concatenate(outs, axis=0), carry


def _ada_kernel(c_ref, w_ref, b_ref, o_ref):
    c = c_ref[...]
    s = (c * jax.nn.sigmoid(c)).astype(BF16)
    o_ref[...] = _dot(s, w_ref[...].astype(BF16)) + b_ref[...]


def _adaln(c_all, w_ada, b_ada):
    n = c_all.shape[0]
    tn = D_MODEL
    return pl.pallas_call(
        _ada_kernel,
        grid=(N_MOD * D_MODEL // tn,),
        in_specs=[pl.BlockSpec((n, D_MODEL), lambda j: (0, 0)),
                  pl.BlockSpec((D_MODEL, tn), lambda j: (0, j)),
                  pl.BlockSpec((1, tn), lambda j: (0, j))],
        out_specs=pl.BlockSpec((n, tn), lambda j: (0, j)),
        out_shape=jax.ShapeDtypeStruct((n, N_MOD * D_MODEL), F32),
        compiler_params=_cparams(1),
        name="adaln",
    )(c_all, w_ada, b_ada.reshape(1, -1))


def _mod_spec(arr, tm):
    if arr.shape[1] == 1:
        return pl.BlockSpec((None, 1, D_MODEL), lambda b, t: (b, 0, 0))
    return pl.BlockSpec((None, tm, D_MODEL), lambda b, t: (b, t, 0))


def _ffn_kernel(x_ref, g_ref, sh_ref, sc_ref, gt_ref, wup_ref, wdn_ref, gfin_ref, o_ref, gs_ref, *, final):
    x = x_ref[...]
    h = (_rms(x, g_ref[...]) * (1.0 + sc_ref[...]) + sh_ref[...]).astype(BF16)
    for c in range(D_FF // FF_CHUNK):
        lo = c * FF_CHUNK
        a = _dot(h, wup_ref[:, lo:lo + FF_CHUNK])
        b = _dot(h, wup_ref[:, D_FF + lo:D_FF + lo + FF_CHUNK])
        gs_ref[:, lo:lo + FF_CHUNK] = (a * jax.nn.sigmoid(a) * b).astype(BF16)
    y = _dot(gs_ref[...], wdn_ref[...])
    out = x + 0.5 * gt_ref[...] * y
    if final:
        out = _rms(out, gfin_ref[...])
    o_ref[...] = out


def _ffn(x, g, sh, sc, gt, w_up, w_dn, g_final, *, tm, final):
    nb, t, _ = x.shape
    xspec = pl.BlockSpec((None, tm, D_MODEL), lambda b, i: (b, i, 0))
    return pl.pallas_call(
        functools.partial(_ffn_kernel, final=final),
        grid=(nb, t // tm),
        in_specs=[xspec, _const_spec((1, D_MODEL)), _mod_spec(sh, tm), _mod_spec(sc, tm), _mod_spec(gt, tm),
                  _const_spec(w_up.shape), _const_spec(w_dn.shape), _const_spec((1, D_MODEL))],
        out_specs=xspec,
        out_shape=jax.ShapeDtypeStruct(x.shape, F32),
        scratch_shapes=[pltpu.VMEM((tm, D_FF), BF16)],
        compiler_params=_cparams(2),
        name="ffn_final" if final else "ffn",
    )(x, g, sh, sc, gt, w_up, w_dn, g_final)


def _mixin_kernel(x_ref, g_ref, sh_ref, sc_ref, wqkv_ref, wf_ref, wu_ref, bf_ref, gq_ref, gk_ref,
                  mblk_ref, ltri_ref, sk_ref,
                  qs_ref, k_ref, v_ref, logf_ref, u_ref, *rest, prompt):
    if prompt:
        kx_ref, vx_ref, p_ref, carry_ref = rest
    else:
        (floc_ref,) = rest
    x = x_ref[...]
    h = (_rms(x, g_ref[...]) * (1.0 + sc_ref[...]) + sh_ref[...]).astype(BF16)

    qkv = _dot(h, wqkv_ref[...])
    q = qkv[:, 0:D_ATT]
    k = qkv[:, D_ATT:2 * D_ATT]
    v = qkv[:, 2 * D_ATT:3 * D_ATT]
    mblk = mblk_ref[...]
    qn = q * lax.rsqrt(_dot((q * q).astype(BF16), mblk) + EPS) * gq_ref[...]
    kn = k * lax.rsqrt(_dot((k * k).astype(BF16), mblk) + EPS) * gk_ref[...]
    qs_ref[...] = (qn * (LOG2E * HEAD_DIM ** -0.5)).astype(BF16)
    k_ref[...] = kn
    v_ref[...] = v

    au = _dot(h, wu_ref[...])
    u_ref[...] = au[:, 0:D_CONV] * jax.nn.sigmoid(au[:, D_CONV:2 * D_CONV])

    lf = jax.nn.log_sigmoid(_dot(h, wf_ref[...]) + bf_ref[...])
    logf_ref[...] = lf[:, 0:N_HEADS]

    if prompt:
        @pl.when(pl.program_id(1) == 0)
        def _():
            carry_ref[...] = jnp.zeros_like(carry_ref)
        fcum, carry = _cumsum_rows(lf, ltri_ref[...], carry_ref[0:1, :])
        carry_ref[0:1, :] = carry
        p = _pack_forget(fcum * LOG2E)
        p_ref[...] = p
        e = _dot(p, sk_ref[...]).astype(BF16)
        for g in range(N_PAIRS):
            sl = slice(g * LANES, (g + 1) * LANES)
            kx_ref[g, :, 0:LANES] = kn[:, sl].astype(BF16)
            kx_ref[g, :, LANES:2 * LANES] = e[:, sl]
        vt = v.T.astype(BF16)
        ones = jnp.ones((HEAD_DIM, x.shape[0]), BF16)
        for h in range(N_HEADS):
            vx_ref[h, 0:HEAD_DIM, :] = vt[h * HEAD_DIM:(h + 1) * HEAD_DIM]
            vx_ref[h, HEAD_DIM:2 * HEAD_DIM, :] = ones
    else:
        fcum, _ = _cumsum_rows(lf, ltri_ref[...], None)
        floc_ref[...] = fcum


def _mixin(x, g, sh, sc, w_qkv, w_f, w_u, b_f, gq, gk, mblk, ltri, sk, *, tm, prompt):
    nb, t, _ = x.shape
    row = lambda w: pl.BlockSpec((None, tm, w), lambda b, i: (b, i, 0))
    out_shape = [jax.ShapeDtypeStruct((nb, t, D_ATT), BF16),
                 jax.ShapeDtypeStruct((nb, t, D_ATT), F32),
                 jax.ShapeDtypeStruct((nb, t, D_ATT), F32),
                 jax.ShapeDtypeStruct((nb, t, N_HEADS), F32),
                 jax.ShapeDtypeStruct((nb, t, D_CONV), F32)]
    out_specs = [row(D_ATT), row(D_ATT), row(D_ATT), row(N_HEADS), row(D_CONV)]
    scratch = []
    if prompt:
        pair = pl.BlockSpec((None, N_PAIRS, tm, 2 * LANES), lambda b, i: (b, 0, i, 0))
        vspec = pl.BlockSpec((None, N_HEADS, None, 2 * HEAD_DIM, tm), lambda b, i: (b, 0, i, 0, 0))
        out_shape += [jax.ShapeDtypeStruct((nb, N_PAIRS, t, 2 * LANES), BF16),
                      jax.ShapeDtypeStruct((nb, N_HEADS, t // tm, 2 * HEAD_DIM, tm), BF16),
                      jax.ShapeDtypeStruct((nb, t, LANES), BF16)]
        out_specs += [pair, vspec, row(LANES)]
        scratch = [pltpu.VMEM((SUBLANES, LANES), F32)]
    else:
        out_shape += [jax.ShapeDtypeStruct((nb, t, LANES), F32)]
        out_specs += [row(LANES)]
    return pl.pallas_call(
        functools.partial(_mixin_kernel, prompt=prompt),
        grid=(nb, t // tm),
        in_specs=[row(D_MODEL), _const_spec((1, D_MODEL)), _mod_spec(sh, tm), _mod_spec(sc, tm),
                  _const_spec(w_qkv.shape), _const_spec(w_f.shape), _const_spec(w_u.shape),
                  _const_spec(b_f.shape), _const_spec(gq.shape), _const_spec(gk.shape),
                  _const_spec(mblk.shape), _const_spec(ltri.shape), _const_spec(sk.shape)],
        out_specs=out_specs,
        out_shape=out_shape,
        scratch_shapes=scratch,
        compiler_params=_cparams(2),
        name="mixin_prompt" if prompt else "mixin_sample",
    )(x, g, sh, sc, w_qkv, w_f, w_u, b_f, gq, gk, mblk, ltri, sk)


def _attn_kernel(q_ref, p_ref, sq_ref, kx_ref, vx_ref, o_ref, qx_ref, s0_ref, s1_ref, m_ref, acc_ref, *, tq, tk):
    i = pl.program_id(2)
    first = lax.broadcasted_iota(jnp.int32, (tq, LANES), 1) < HEAD_DIM
    qp = q_ref[...].astype(F32)
    p = p_ref[...]
    for hd in range(2):
        qm = jnp.where(first if hd == 0 else jnp.logical_not(first), qp, 0.0)
        qx_ref[hd] = jnp.concatenate([qm, _dot(p, sq_ref[hd])], axis=1).T.astype(BF16)
    m_ref[...] = jnp.full_like(m_ref, NEG_BIG)
    acc_ref[...] = jnp.zeros_like(acc_ref)
    s_refs = (s0_ref, s1_ref)

    def scores(j, slot, qlo=0):
        kx = kx_ref[pl.ds(pl.multiple_of(j * tk, tk), tk), :]
        for hd in range(2):
            s_refs[slot][hd, :, qlo:] = _dot(kx, qx_ref[hd, :, qlo:])

    def update(j, slot, masked, qlo=0):
        for hd in range(2):
            s = s_refs[slot][hd, :, qlo:]
            if masked:
                key = lax.broadcasted_iota(jnp.int32, s.shape, 0)
                qry = lax.broadcasted_iota(jnp.int32, s.shape, 1)
                s = jnp.where(key <= qry, s, NEG_BIG)
            m_old = m_ref[hd, :, qlo:]
            m_new = jnp.maximum(m_old, jnp.max(s, axis=0, keepdims=True))
            pr = jnp.exp2(s - m_new).astype(BF16)
            acc_ref[hd, :, qlo:] = acc_ref[hd, :, qlo:] * jnp.exp2(m_old - m_new) + _dot(vx_ref[hd, j], pr)
            m_ref[hd, :, qlo:] = m_new

    scores(0, 0)

    def body(jj, carry):
        j = 2 * jj
        scores(j + 1, 1)
        update(j, 0, False)
        scores(j + 2, 0)
        update(j + 1, 1, False)
        return carry

    lax.fori_loop(0, i, body, 0)
    scores(2 * i + 1, 1, tk)
    update(2 * i, 0, True)
    update(2 * i + 1, 1, True, tk)

    out_t = jnp.concatenate([acc_ref[hd, 0:HEAD_DIM, :] / acc_ref[hd, HEAD_DIM:HEAD_DIM + 1, :] for hd in range(2)],
                            axis=0)
    o_ref[...] = out_t.T.astype(o_ref.dtype)


def _attention(qs, p, sq, kx, vx, *, tq):
    nb, t, _ = qs.shape
    nk, tk = vx.shape[2], vx.shape[4]
    assert tq == 2 * tk and t % tq == 0
    blk = lambda b, g, i: (b, i, g)
    return pl.pallas_call(
        functools.partial(_attn_kernel, tq=tq, tk=tk),
        grid=(nb, N_PAIRS, t // tq),
        in_specs=[pl.BlockSpec((None, tq, LANES), blk),
                  pl.BlockSpec((None, tq, LANES), lambda b, g, i: (b, i, 0)),
                  pl.BlockSpec((2, LANES, LANES), lambda b, g, i: (g, 0, 0)),
                  pl.BlockSpec((None, None, t, 2 * LANES), lambda b, g, i: (b, g, 0, 0)),
                  pl.BlockSpec((None, 2, nk, 2 * HEAD_DIM, tk), lambda b, g, i: (b, g, 0, 0, 0))],
        out_specs=pl.BlockSpec((None, tq, LANES), blk),
        out_shape=jax.ShapeDtypeStruct((nb, t, D_ATT), BF16),
        scratch_shapes=[pltpu.VMEM((2, 2 * LANES, tq), BF16),
                        pltpu.VMEM((2, tk, tq), F32), pltpu.VMEM((2, tk, tq), F32),
                        pltpu.VMEM((2, 1, tq), F32), pltpu.VMEM((2, 2 * HEAD_DIM, tq), F32)],
        compiler_params=_cparams(3),
        name="fox_attention",
    )(qs, p, sq, kx, vx)


def _cumsum_kernel(x_ref, ltri_ref, o_ref):
    out, _ = _cumsum_rows(x_ref[...], ltri_ref[...], jnp.zeros((1, LANES), F32))
    o_ref[...] = out


def _cumsum(x, ltri):
    return pl.pallas_call(
        _cumsum_kernel,
        out_shape=jax.ShapeDtypeStruct(x.shape, F32),
        compiler_params=pltpu.CompilerParams(vmem_limit_bytes=VMEM_LIMIT),
        name="cache_cumsum",
    )(x, ltri)


def _sattn_kernel(q_ref, kn_ref, vn_ref, fl_ref, ck_ref, cv_ref, fc_ref, sks_ref, sqs_ref, o_ref, *, tn):
    fc = fc_ref[...]
    fn = fl_ref[...] + fc[fc.shape[0] - 1:, :]
    pc = _pack_forget(fc * LOG2E)
    pn = _pack_forget(fn * LOG2E)
    sks = sks_ref[...]
    kc = jnp.concatenate([ck_ref[...].astype(BF16), _dot(pc, sks).astype(BF16)], axis=1)
    kn = jnp.concatenate([kn_ref[...].astype(BF16), _dot(pn, sks).astype(BF16)], axis=1)
    q = q_ref[...]
    head = lax.broadcasted_iota(jnp.int32, q.shape, 1) // HEAD_DIM
    zero = jnp.zeros_like(q)
    qm = jnp.concatenate(
        [jnp.concatenate([jnp.where(head == h, q, zero), _dot(pn, sqs_ref[h]).astype(BF16)], axis=1)
         for h in range(N_HEADS)], axis=0)
    sc = _dot_nt(qm, kc)
    sn = _dot_nt(qm, kn)
    r = lax.broadcasted_iota(jnp.int32, sn.shape, 0) % tn
    c = lax.broadcasted_iota(jnp.int32, sn.shape, 1)
    sn = jnp.where(c <= r, sn, NEG_BIG)
    m = jnp.maximum(jnp.max(sc, axis=-1, keepdims=True), jnp.max(sn, axis=-1, keepdims=True))
    ec = jnp.exp2(sc - m)
    en = jnp.exp2(sn - m)
    den = jnp.sum(ec, axis=-1, keepdims=True) + jnp.sum(en, axis=-1, keepdims=True)
    o = (_dot(ec.astype(BF16), cv_ref[...].astype(BF16)) + _dot(en.astype(BF16), vn_ref[...].astype(BF16))) / den
    ohead = lax.broadcasted_iota(jnp.int32, (tn, D_ATT), 1) // HEAD_DIM
    out = jnp.zeros((tn, D_ATT), F32)
    for h in range(N_HEADS):
        out = out + jnp.where(ohead == h, o[h * tn:(h + 1) * tn], 0.0)
    o_ref[...] = out.astype(o_ref.dtype)


def _sample_attention(qs, kn, vn, floc, cache_k, cache_v, fc3, sks, sqs):
    nb, tn, _ = qs.shape
    past = cache_k.shape[1]
    per_b = lambda r, w: pl.BlockSpec((None, r, w), lambda b: (b, 0, 0))
    return pl.pallas_call(
        functools.partial(_sattn_kernel, tn=tn),
        grid=(nb,),
        in_specs=[per_b(tn, D_ATT), per_b(tn, D_ATT), per_b(tn, D_ATT), per_b(tn, LANES),
                  per_b(past, D_ATT), per_b(past, D_ATT), per_b(past, LANES),
                  _const_spec(sks.shape), _const_spec(sqs.shape)],
        out_specs=per_b(tn, D_ATT),
        out_shape=jax.ShapeDtypeStruct((nb, tn, D_ATT), BF16),
        compiler_params=_cparams(1),
        name="sample_attention",
    )(qs, kn, vn, floc, cache_k, cache_v, fc3, sks, sqs)


def _mixout_kernel(x_ref, o_ref, u_ref, halo_ref, cw_ref, cb_ref, lg_ref, lb_ref, wa_ref, wc_ref, gt_ref,
                   out_ref, ubuf_ref, ybuf_ref, *, tm, rc, zero_first):
    halo = halo_ref[...]
    if zero_first:
        halo = jnp.where(pl.program_id(1) > 0, halo, 0.0)
    ubuf_ref[0:HALO, :] = halo
    ubuf_ref[HALO:HALO + tm, :] = u_ref[...]
    lead = HALO - CONV_STATE
    for r0 in range(0, tm, rc):
        for c0 in range(0, D_CONV, LANES):
            acc = jnp.broadcast_to(cb_ref[:, c0:c0 + LANES], (rc, LANES))
            win = ubuf_ref[r0:r0 + rc + HALO, c0:c0 + LANES]
            for sh in range(SUBLANES):
                ush = win if sh == 0 else pltpu.roll(win, rc + HALO - sh, 0)
                for j in range(CONV_WIDTH):
                    if (lead + j) % SUBLANES == sh:
                        a = (lead + j) // SUBLANES * SUBLANES
                        acc = acc + cw_ref[j:j + 1, c0:c0 + LANES] * ush[a:a + rc]
            ybuf_ref[r0:r0 + rc, c0:c0 + LANES] = acc
    y = ybuf_ref[...]
    mu = jnp.mean(y, axis=-1, keepdims=True)
    yc = y - mu
    var = jnp.mean(yc * yc, axis=-1, keepdims=True)
    yn = yc * lax.rsqrt(var + EPS) * lg_ref[...] + lb_ref[...]
    ys = (yn * jax.nn.sigmoid(yn)).astype(BF16)
    m = _dot(o_ref[...], wa_ref[...]) + _dot(ys, wc_ref[...])
    out_ref[...] = x_ref[...] + gt_ref[...] * m


def _mixout(x, o, u, halo_arr, halo_map, cw, cb, lg, lb, wa, wc, gt, *, tm, zero_first):
    nb, t, _ = x.shape
    row = lambda w: pl.BlockSpec((None, tm, w), lambda b, i: (b, i, 0))
    rc = min(tm, 128)
    return pl.pallas_call(
        functools.partial(_mixout_kernel, tm=tm, rc=rc, zero_first=zero_first),
        grid=(nb, t // tm),
        in_specs=[row(D_MODEL), row(D_ATT), row(D_CONV),
                  pl.BlockSpec((None, HALO, D_CONV), halo_map),
                  _const_spec(cw.shape), _const_spec(cb.shape), _const_spec(lg.shape), _const_spec(lb.shape),
                  _const_spec(wa.shape), _const_spec(wc.shape), _mod_spec(gt, tm)],
        out_specs=row(D_MODEL),
        out_shape=jax.ShapeDtypeStruct(x.shape, F32),
        scratch_shapes=[pltpu.VMEM((HALO + tm, D_CONV), F32), pltpu.VMEM((tm, D_CONV), F32)],
        compiler_params=_cparams(2),
        name="mixout",
    )(x, o, u, halo_arr, cw, cb, lg, lb, wa, wc, gt)


def _selection_constants():
    hid = np.arange(D_ATT) // HEAD_DIM
    mblk = (hid[:, None] == hid[None, :]).astype(np.float32) / HEAD_DIM
    sk = np.zeros((LANES, N_PAIRS * LANES), np.float32)
    for g in range(N_PAIRS):
        base = g * LANES
        sk[ONE_LANE, base:base + F_PARTS] = 1.0
        for j in range(F_PARTS):
            sk[N_HEADS * j + 2 * g, base + F_PARTS + j] = -1.0
            sk[N_HEADS * j + 2 * g + 1, base + 2 * F_PARTS + j] = -1.0
    sq = np.zeros((N_HEADS, LANES, LANES), np.float32)
    for h in range(N_HEADS):
        for j in range(F_PARTS):
            sq[h, N_HEADS * j + h, j] = 1.0
            sq[h, ONE_LANE, F_PARTS + F_PARTS * (h % 2) + j] = 1.0
    sks = np.zeros((LANES, LANES), np.float32)
    sks[ONE_LANE, 0:F_PARTS] = 1.0
    sqs = np.zeros((N_HEADS, LANES, LANES), np.float32)
    for h in range(N_HEADS):
        for j in range(F_PARTS):
            sks[N_HEADS * j + h, F_PARTS + F_PARTS * h + j] = -1.0
            sqs[h, N_HEADS * j + h, j] = 1.0
            sqs[h, ONE_LANE, F_PARTS + F_PARTS * h + j] = 1.0
    r = np.arange(CUM_BLK)
    ltri = (r[None, :] <= r[:, None]).astype(np.float32)
    return mblk, sk, sq, sks, sqs, ltri


def _rep_lanes(a):
    rep = jnp.concatenate([a] * F_PARTS, axis=-1)
    pad = [(0, 0)] * (a.ndim - 1) + [(0, LANES - F_PARTS * N_HEADS)]
    return jnp.pad(rep, pad)


def kernel(x_prompt, x_sample, c_prompt, c_sample, cache_k, cache_v, cache_logf, state_conv, w_ada, b_ada, g_ffn1, w_up1, w_down1, g_mix, w_in, b_f, g_q, g_k, conv_w, conv_b, conv_ln_g, conv_ln_b, w_out, g_ffn2, w_up2, w_down2, g_final):
    depth = w_ada.shape[0]
    assert depth == 1, "single-layer trunk"
    nbp, seq, _ = x_prompt.shape
    nbs, tn, _ = x_sample.shape
    past = cache_k.shape[2]
    assert (nbs * tn) % CUM_BLK == 0 and CUM_BLK % tn == 0 and past % CUM_BLK == 0

    mblk_np, sk_np, sq_np, sks_np, sqs_np, ltri_np = _selection_constants()
    mblk, sk, sq, sks, sqs, ltri = (jnp.asarray(a, BF16) for a in (mblk_np, sk_np, sq_np, sks_np, sqs_np, ltri_np))
    r = np.arange(CUM_BLK)
    ltri_s = jnp.asarray(ltri_np * (r[:, None] // tn == r[None, :] // tn), BF16)

    l = 0
    row = lambda a: a[l].reshape(1, -1)
    w_up1b, w_dn1b = w_up1[l].astype(BF16), w_down1[l].astype(BF16)
    w_up2b, w_dn2b = w_up2[l].astype(BF16), w_down2[l].astype(BF16)
    w_qkv = w_in[l][:, :3 * D_ATT].astype(BF16)
    w_f = _rep_lanes(w_in[l][:, 3 * D_ATT:3 * D_ATT + N_HEADS]).astype(BF16)
    w_u = w_in[l][:, 3 * D_ATT + N_HEADS:].astype(BF16)
    b_f3 = _rep_lanes(b_f[l].reshape(1, -1))
    gq = jnp.tile(g_q[l], N_HEADS).reshape(1, -1)
    gk = jnp.tile(g_k[l], N_HEADS).reshape(1, -1)
    w_oa, w_oc = w_out[l][:D_ATT].astype(BF16), w_out[l][D_ATT:].astype(BF16)
    cw = jnp.pad(conv_w[l], ((0, 1), (0, 0)))

    mod = _adaln(jnp.concatenate([c_prompt, c_sample], axis=0), w_ada[l], b_ada[l])
    mods = [mod[:, i * D_MODEL:(i + 1) * D_MODEL] for i in range(N_MOD)]
    mp = [m[:nbp].reshape(nbp, 1, D_MODEL) for m in mods]
    ms_tok = [jnp.repeat(m[nbp:], tn, axis=0).reshape(1, nbs * tn, D_MODEL) for m in mods]
    ms_b = [m[nbp:].reshape(nbs, 1, D_MODEL) for m in mods]

    tm = 512
    xp = _ffn(x_prompt, row(g_ffn1), mp[0], mp[1], mp[2], w_up1b, w_dn1b, row(g_final), tm=tm, final=False)
    qs, kp, vp, fp, up, kx, vx, pk = _mixin(xp, row(g_mix), mp[3], mp[4], w_qkv, w_f, w_u, b_f3, gq, gk,
                                            mblk, ltri, sk, tm=tm, prompt=True)
    op = _attention(qs, pk, sq, kx, vx, tq=2 * tm)
    halo_blocks = tm // HALO
    xp = _mixout(xp, op, up, up, lambda b, i: (b, jnp.maximum(i * halo_blocks - 1, 0), 0),
                 cw, row(conv_b), row(conv_ln_g), row(conv_ln_b), w_oa, w_oc, mp[5], tm=tm, zero_first=True)
    yp = _ffn(xp, row(g_ffn2), mp[6], mp[7], mp[8], w_up2b, w_dn2b, row(g_final), tm=tm, final=True)

    ns = nbs * tn
    xs = x_sample.reshape(1, ns, D_MODEL)
    xs = _ffn(xs, row(g_ffn1), ms_tok[0], ms_tok[1], ms_tok[2], w_up1b, w_dn1b, row(g_final), tm=ns, final=False)
    qss, ks, vs, fs, us, floc = _mixin(xs, row(g_mix), ms_tok[3], ms_tok[4], w_qkv, w_f, w_u, b_f3, gq, gk,
                                       mblk, ltri_s, sk, tm=ns, prompt=False)
    fc = _cumsum(cache_logf[l].transpose(1, 0, 2).reshape(past, nbs * N_HEADS), ltri)
    fc3 = _rep_lanes(fc.reshape(past, nbs, N_HEADS).transpose(1, 0, 2))
    per_b = lambda a: a.reshape(nbs, tn, a.shape[-1])
    os_ = _sample_attention(per_b(qss), per_b(ks), per_b(vs), per_b(floc), cache_k[l].reshape(nbs, past, D_ATT),
                            cache_v[l].reshape(nbs, past, D_ATT), fc3, sks, sqs)
    state = jnp.pad(state_conv[l], ((0, 0), (HALO - CONV_STATE, 0), (0, 0)))
    xs = _mixout(per_b(xs), os_, per_b(us), state, lambda b, i: (b, 0, 0),
                 cw, row(conv_b), row(conv_ln_g), row(conv_ln_b), w_oa, w_oc, ms_b[5], tm=tn, zero_first=False)
    ys = _ffn(xs.reshape(1, ns, D_MODEL), row(g_ffn2), ms_tok[6], ms_tok[7], ms_tok[8], w_up2b, w_dn2b,
              row(g_final), tm=ns, final=True)

    heads = lambda a, nb, t: a.reshape(1, nb, t, N_HEADS, HEAD_DIM)
    conv_p = up[:, seq - CONV_STATE:][None]
    conv_s = jnp.concatenate([state_conv[l], per_b(us)], axis=1)[:, tn:][None]
    return (yp, ys.reshape(nbs, tn, D_MODEL),
            heads(kp, nbp, seq), heads(vp, nbp, seq), fp[None], conv_p,
            heads(ks, nbs, tn), heads(vs, nbs, tn), per_b(fs)[None], conv_s)
```
